```python
import math
import jax, jax.numpy as jnp
from jax import lax
import numpy as np

D_MODEL = 1024
BATCH = 8
SEQ = 8192
DEPTH = 4

N_GROUPS = 4
GROUP_WIDTH = D_MODEL // N_GROUPS
HEADS = 4
HEAD_DIM = GROUP_WIDTH // HEADS
N_SLICES = 11
IN_WIDTH = N_SLICES * GROUP_WIDTH
CHUNK = 128
ROPE_BASE = 10000.0
LRU_C = 8.0
LRU_CONV_WIDTH = 4
LRU_CONV_PAD = (2, 1)
SC_CONV_WIDTH = 3
SC_CONV_PAD = (1, 1)
D_FF = math.ceil(8 * D_MODEL / 3 / 256) * 256
DEEPNORM_ALPHA = (2 * DEPTH) ** 0.25
DEEPNORM_BETA = (8 * DEPTH) ** -0.25
LN_EPS = 1e-5

kernel_name = 'hybrid_parallel_group_encoder'


def layer_norm(x, g, b):
    xf = x.astype(jnp.float32)
    mu = xf.mean(-1, keepdims=True)
    var = jnp.square(xf - mu).mean(-1, keepdims=True)
    return ((xf - mu) * lax.rsqrt(var + LN_EPS) * g + b).astype(x.dtype)


def depthwise_conv(x, w, pad):
    return lax.conv_general_dilated(
        x, w[:, None, :], window_strides=(1,), padding=[pad],
        dimension_numbers=('NWC', 'WIO', 'NWC'), feature_group_count=x.shape[-1])


def gmlp_mixer(u, v, ln_g, ln_b, w_s, b_s):
    bsz, s, w = u.shape
    u = jax.nn.gelu(u)
    v = layer_norm(jax.nn.gelu(v), ln_g, ln_b)
    vc = v.reshape(bsz, s // CHUNK, CHUNK, HEADS, HEAD_DIM)
    mixed = jnp.einsum('hpq,bnqhd->bnphd', w_s, vc) + b_s.T[None, None, :, :, None]
    return u * mixed.reshape(bsz, s, w)


def rotary(x, cos, sin):
    x1, x2 = jnp.split(x, 2, axis=-1)
    return jnp.concatenate([x1 * cos - x2 * sin, x2 * cos + x1 * sin], axis=-1)


def retention_mixer(q, k, v, g, positions, gn_g, gn_b):
    bsz, s, w = q.shape
    nc, f32 = s // CHUNK, jnp.float32
    inv_freq = ROPE_BASE ** (-jnp.arange(0, HEAD_DIM, 2, dtype=f32) / HEAD_DIM)
    ang = positions.astype(f32)[..., None] * inv_freq
    cos, sin = jnp.cos(ang)[:, :, None, :], jnp.sin(ang)[:, :, None, :]
    shp = (bsz, s, HEADS, HEAD_DIM)
    qh = rotary(q.astype(f32).reshape(shp), cos, sin)
    kh = rotary(k.astype(f32).reshape(shp), cos, sin) * HEAD_DIM ** -0.5
    vh = v.astype(f32).reshape(shp)
    cshp = (bsz, nc, CHUNK, HEADS, HEAD_DIM)
    qc, kc, vc = qh.reshape(cshp), kh.reshape(cshp), vh.reshape(cshp)
    log_gamma = jnp.log1p(-jnp.exp2(-5.0 - jnp.arange(HEADS, dtype=f32)))
    idx = jnp.arange(CHUNK, dtype=f32)
    intra = jnp.exp(log_gamma[:, None, None] * jnp.abs(idx[:, None] - idx[None, :]))
    scores = jnp.einsum('bnihd,bnjhd->bnhij', qc, kc) * intra
    out = jnp.einsum('bnhij,bnjhe->bnihe', scores, vc)
    k_dec_f = jnp.exp(log_gamma[None, :] * (CHUNK - 1 - idx)[:, None])
    k_dec_b = jnp.exp(log_gamma[None, :] * idx[:, None])
    u_f = jnp.einsum('bnjhd,bnjhe->nbhde', kc * k_dec_f[None, None, :, :, None], vc)
    u_b = jnp.einsum('bnjhd,bnjhe->nbhde', kc * k_dec_b[None, None, :, :, None], vc)
    chunk_decay = jnp.exp(log_gamma * CHUNK)[None, :, None, None]

    def step(state, u):
        return chunk_decay * state + u, state

    zero = jnp.zeros((bsz, HEADS, HEAD_DIM, HEAD_DIM), f32)
    _, r_prev = lax.scan(step, zero, u_f)
    _, l_next = lax.scan(step, zero, u_b, reverse=True)
    q_dec_f = jnp.exp(log_gamma[None, :] * (idx + 1.0)[:, None])
    q_dec_b = jnp.exp(log_gamma[None, :] * (CHUNK - idx)[:, None])
    out = (out
           + jnp.einsum('bnihd,nbhde->bnihe', qc * q_dec_f[None, None, :, :, None], r_prev)
           + jnp.einsum('bnihd,nbhde->bnihe', qc * q_dec_b[None, None, :, :, None], l_next))
    out = out.reshape(shp)
    mu = out.mean(-1, keepdims=True)
    var = jnp.square(out - mu).mean(-1, keepdims=True)
    out = ((out - mu) * lax.rsqrt(var + LN_EPS)).reshape(bsz, s, w) * gn_g + gn_b
    return (jax.nn.silu(g.astype(f32)) * out).astype(q.dtype)


def linear_scan(a, b):
    def combine(left, right):
        a_l, b_l = left
        a_r, b_r = right
        return a_l * a_r, a_r * b_l + b_r
    _, h = lax.associative_scan(combine, (a, b), axis=1)
    return h


def rglru_mixer(xr, gate, conv_w, conv_b, w_a, b_a, w_x, b_x, lam):
    bsz, s, w = xr.shape
    f32 = jnp.float32
    xf = (depthwise_conv(xr, conv_w, LRU_CONV_PAD) + conv_b).astype(f32)
    xb = xf.reshape(bsz, s, HEADS, HEAD_DIM)
    r = jax.nn.sigmoid(jnp.einsum('bshi,zhij->zbshj', xb, w_a).reshape(2, bsz, s, w) + b_a[:, None, None, :])
    i = jax.nn.sigmoid(jnp.einsum('bshi,zhij->zbshj', xb, w_x).reshape(2, bsz, s, w) + b_x[:, None, None, :])
    log_a = -LRU_C * jax.nn.softplus(-lam.astype(f32))[:, None, None, :] * r
    a = jnp.exp(log_a)
    b = xf[None] * i * jnp.sqrt(-jnp.expm1(2.0 * log_a))
    h_fwd = linear_scan(a[0], b[0])
    h_bwd = jnp.flip(linear_scan(jnp.flip(a[1], 1), jnp.flip(b[1], 1)), 1)
    return ((h_fwd + h_bwd) * jax.nn.gelu(gate.astype(f32))).astype(gate.dtype)


def short_conv_mixer(bg, cg, h, conv_w):
    return bg * depthwise_conv(cg * h, conv_w, SC_CONV_PAD)


def setup_inputs(seed: int = 0) -> dict:
    key = jax.random.key(seed)
    ks = jax.random.split(key, 24)
    L, D, W = DEPTH, D_MODEL, GROUP_WIDTH
    f32 = jnp.float32
    nrm = lambda k, shape, scale: jax.random.normal(k, shape, f32) * scale
    gain = lambda k, shape: 1.0 + 0.02 * jax.random.normal(k, shape, f32)
    u = jax.random.uniform(ks[12], (L, 2, W), f32, 0.9, 0.999)
    sig = u ** (1.0 / LRU_C)
    lru_lambda = jnp.log(sig) - jnp.log1p(-sig)
    return {
        'x': jax.random.normal(ks[0], (BATCH, SEQ, D), f32),
        'positions': jnp.broadcast_to(jnp.arange(SEQ, dtype=jnp.int32), (BATCH, SEQ)),
        'w_in': nrm(ks[1], (L, D, IN_WIDTH), D ** -0.5),
        'gmlp_ln_g': gain(ks[2], (L, W)),
        'gmlp_ln_b': nrm(ks[3], (L, W), 0.02),
        'gmlp_ws': nrm(ks[4], (L, HEADS, CHUNK, CHUNK), CHUNK ** -0.5),
        'gmlp_bs': gain(ks[5], (L, HEADS, CHUNK)),
        'ret_gn_g': gain(ks[6], (L, W)),
        'ret_gn_b': nrm(ks[7], (L, W), 0.02),
        'lru_conv_w': nrm(ks[8], (L, LRU_CONV_WIDTH, W), LRU_CONV_WIDTH ** -0.5),
        'lru_conv_b': nrm(ks[9], (L, W), 0.02),
        'lru_wa': nrm(ks[10], (L, 2, HEADS, HEAD_DIM, HEAD_DIM), HEAD_DIM ** -0.5),
        'lru_ba': nrm(ks[11], (L, 2, W), 0.02),
        'lru_wx': nrm(ks[13], (L, 2, HEADS, HEAD_DIM, HEAD_DIM), HEAD_DIM ** -0.5),
        'lru_bx': nrm(ks[14], (L, 2, W), 0.02),
        'lru_lambda': lru_lambda,
        'sc_conv_w': nrm(ks[15], (L, SC_CONV_WIDTH, W), SC_CONV_WIDTH ** -0.5),
        'w_out': nrm(ks[16], (L, D, D), D ** -0.5 * DEEPNORM_BETA),
        'ln1_g': gain(ks[17], (L, D)),
        'ln1_b': nrm(ks[18], (L, D), 0.02),
        'ffn_wg': nrm(ks[19], (L, D, D_FF), D ** -0.5),
        'ffn_wu': nrm(ks[20], (L, D, D_FF), D ** -0.5),
        'ffn_wd': nrm(ks[21], (L, D_FF, D), D_FF ** -0.5 * DEEPNORM_BETA),
        'ln2_g': gain(ks[22], (L, D)),
        'ln2_b': nrm(ks[23], (L, D), 0.02),
    }


def reference(x, positions, w_in, gmlp_ln_g, gmlp_ln_b, gmlp_ws, gmlp_bs, ret_gn_g, ret_gn_b,
              lru_conv_w, lru_conv_b, lru_wa, lru_ba, lru_wx, lru_bx, lru_lambda, sc_conv_w,
              w_out, ln1_g, ln1_b, ffn_wg, ffn_wu, ffn_wd, ln2_g, ln2_b):
    for l in range(DEPTH):
        z = jnp.einsum('bsd,dk->bsk', x, w_in[l])
        (a_u, a_v, r_q, r_k, r_v, r_g, c_x, c_gate, d_b, d_c, d_h) = jnp.split(z, N_SLICES, axis=-1)
        y = jnp.concatenate([
            gmlp_mixer(a_u, a_v, gmlp_ln_g[l], gmlp_ln_b[l], gmlp_ws[l], gmlp_bs[l]),
            retention_mixer(r_q, r_k, r_v, r_g, positions, ret_gn_g[l], ret_gn_b[l]),
            rglru_mixer(c_x, c_gate, lru_conv_w[l], lru_conv_b[l], lru_wa[l], lru_ba[l],
                        lru_wx[l], lru_bx[l], lru_lambda[l]),
            short_conv_mixer(d_b, d_c, d_h, sc_conv_w[l]),
        ], axis=-1)
        x = layer_norm(DEEPNORM_ALPHA * x + y @ w_out[l], ln1_g[l], ln1_b[l])
        f = (jax.nn.silu(x @ ffn_wg[l]) * (x @ ffn_wu[l])) @ ffn_wd[l]
        x = layer_norm(DEEPNORM_ALPHA * x + f, ln2_g[l], ln2_b[l])
    return x
```

```python
import functools
import math

import numpy as np
import jax
import jax.numpy as jnp
from jax import lax
from jax.experimental import pallas as pl
from jax.experimental.pallas import tpu as pltpu

F32 = jnp.float32
BF16 = jnp.bfloat16

GROUP_WIDTH = 256
HEADS = 4
HEAD_DIM = 64
HALF_DIM = HEAD_DIM // 2
N_SLICES = 11
CHUNK = 128
ROPE_BASE = 10000.0
LRU_C = 8.0
LN_EPS = 1e-5

SUBLANES = 8
TILE = 512
HALO = 16
VMEM_LIMIT_BYTES = 56 * 1024 * 1024

P_YA, P_Q, P_K, P_V, P_G, P_XF, P_GATE, P_HB, P_OB, P_YD = range(10)
PACK_COLS = 10 * GROUP_WIDTH


def _cols(i, n=1):
    return slice(i * GROUP_WIDTH, (i + n) * GROUP_WIDTH)


def _dot(a, b):
    return jnp.dot(a, b, preferred_element_type=F32)


def _dot_tn(a, b):
    return lax.dot_general(a, b, (((0,), (0,)), ((), ())), preferred_element_type=F32)


def _dot_nt(a, b):
    return lax.dot_general(a, b, (((1,), (1,)), ((), ())), preferred_element_type=F32)


def _layer_norm(x, g, b):
    mu = jnp.mean(x, axis=-1, keepdims=True)
    d = x - mu
    var = jnp.mean(d * d, axis=-1, keepdims=True)
    return d * lax.rsqrt(var + LN_EPS) * g + b


def _softplus(x):
    return jnp.maximum(x, 0.0) + jnp.log1p(jnp.exp(-jnp.abs(x)))


def _lane_head_masks(permuted):
    lane = lax.broadcasted_iota(jnp.int32, (1, GROUP_WIDTH), 1)
    if permuted:
        head = (lane & (GROUP_WIDTH // 2 - 1)) >> 5
    else:
        head = lane >> 6
    return [head == h for h in range(HEADS)]


def _linear_scan(a, b, carry, reverse):
    n, w = a.shape
    nblk = n // SUBLANES
    a3 = a.reshape(nblk, SUBLANES, w)
    b3 = b.reshape(nblk, SUBLANES, w)
    sub = lax.broadcasted_iota(jnp.int32, (nblk, SUBLANES, w), 1)
    s = 1
    while s < SUBLANES:
        if reverse:
            a_s = pltpu.roll(a3, SUBLANES - s, 1)
            b_s = pltpu.roll(b3, SUBLANES - s, 1)
            valid = sub < SUBLANES - s
        else:
            a_s = pltpu.roll(a3, s, 1)
            b_s = pltpu.roll(b3, s, 1)
            valid = sub >= s
        b3 = b3 + jnp.where(valid, a3 * b_s, 0.0)
        a3 = jnp.where(valid, a3 * a_s, a3)
        s *= 2
    out = [None] * nblk
    order = range(nblk - 1, -1, -1) if reverse else range(nblk)
    for i in order:
        h = a3[i] * carry + b3[i]
        carry = h[0:1] if reverse else h[SUBLANES - 1:SUBLANES]
        out[i] = h
    return jnp.concatenate(out, axis=0), carry


def _lru_coeffs(xf, xf_bf16, wg_ref, bg_ref, lam_ref):
    gates = _dot(xf_bf16, wg_ref[...]) + bg_ref[...]
    r = jax.nn.sigmoid(gates[:, :GROUP_WIDTH])
    i = jax.nn.sigmoid(gates[:, GROUP_WIDTH:])
    log_a = (-LRU_C * _softplus(-lam_ref[...])) * r
    a = jnp.exp(log_a)
    b = xf * i * jnp.sqrt(1.0 - a * a)
    return a, b


def _rope_kernel(pos_ref, inv_ref, cos_ref, sin_ref):
    ang = pos_ref[0] * inv_ref[...]
    cos_ref[0] = jnp.cos(ang)
    sin_ref[0] = jnp.sin(ang)


def _rope_tables(positions, inv_row):
    bsz, s = positions.shape
    half = GROUP_WIDTH // 2
    pos_b = jnp.broadcast_to(positions.astype(F32)[..., None], (bsz, s, half))
    spec = pl.BlockSpec((1, TILE, half), lambda b, t: (b, t, 0))
    return pl.pallas_call(
        _rope_kernel,
        grid=(bsz, s // TILE),
        in_specs=[spec, pl.BlockSpec((1, half), lambda b, t: (0, 0))],
        out_specs=[spec, spec],
        out_shape=[jax.ShapeDtypeStruct((bsz, s, half), F32)] * 2,
        name="rope_tables",
    )(pos_b, inv_row)


def _mix_bwd_kernel(x_ref, xp_ref, xn_ref, cos_ref, sin_ref, w_ref,
                    lng_ref, lnb_ref, ws_ref, bs_ref,
                    cw_ref, cb_ref, wg_ref, bg_ref, lam_ref, scw_ref,
                    kdec_ref, qdec_ref, dm_ref, bd_ref,
                    zp_ref, l_ref, hc_ref, cbuf_ref, pbuf_ref):
    t = pl.program_id(1)
    nt = pl.num_programs(1)
    n = x_ref.shape[1]
    nchunk = n // CHUNK

    @pl.when(t == 0)
    def _():
        l_ref[...] = jnp.zeros_like(l_ref)
        hc_ref[...] = jnp.zeros_like(hc_ref)

    has_prev = jnp.where(t != nt - 1, 1.0, 0.0).astype(F32)
    has_next = jnp.where(t != 0, 1.0, 0.0).astype(F32)

    xb = x_ref[0].astype(BF16)
    xpb = xp_ref[0].astype(BF16)
    xnb = xn_ref[0].astype(BF16)

    def proj(xv, i):
        return _dot(xv, w_ref[:, _cols(i)])

    nat = _lane_head_masks(permuted=False)

    u = jax.nn.gelu(proj(xb, 0))
    v = _layer_norm(jax.nn.gelu(proj(xb, 1)), lng_ref[...], lnb_ref[...])
    for c in range(nchunk):
        rows = slice(c * CHUNK, (c + 1) * CHUNK)
        vc = v[rows]
        vstack = jnp.concatenate([jnp.where(m, vc, 0.0) for m in nat], axis=0).astype(BF16)
        mixed = _dot(ws_ref[...], vstack) + bs_ref[...]
        zp_ref[0, rows, _cols(P_YA)] = (u[rows] * mixed).astype(BF16)

    half = GROUP_WIDTH // 2
    cos = cos_ref[0]
    sin = sin_ref[0]

    def rotary(z):
        z1, z2 = z[:, :half], z[:, half:]
        return jnp.concatenate([z1 * cos - z2 * sin, z2 * cos + z1 * sin], axis=1)

    q = rotary(proj(xb, 2))
    k = rotary(proj(xb, 3)) * HEAD_DIM ** -0.5
    vb = proj(xb, 4).astype(BF16)
    zp_ref[0, :, _cols(P_Q)] = q.astype(BF16)
    zp_ref[0, :, _cols(P_K)] = k.astype(BF16)
    zp_ref[0, :, _cols(P_V)] = vb
    zp_ref[0, :, _cols(P_G)] = proj(xb, 5).astype(BF16)
    for c in range(nchunk - 1, -1, -1):
        rows = slice(c * CHUNK, (c + 1) * CHUNK)
        state = l_ref[...]
        ob = _dot((q[rows] * qdec_ref[...]).astype(BF16), state.astype(BF16))
        zp_ref[0, rows, _cols(P_OB)] = ob.astype(BF16)
        upd = _dot_tn((k[rows] * kdec_ref[...]).astype(BF16), vb[rows])
        l_ref[...] = dm_ref[...] * state + bd_ref[...] * upd

    cbuf_ref[0:HALO] = proj(xpb, 6) * has_prev
    cbuf_ref[HALO:HALO + n] = proj(xb, 6)
    cbuf_ref[HALO + n:HALO + n + HALO] = proj(xnb, 6) * has_next
    xf = cb_ref[...]
    for j in range(4):
        xf = xf + cw_ref[j:j + 1, :] * cbuf_ref[pl.ds(HALO + j - 2, n), :]
    xf_bf16 = xf.astype(BF16)
    zp_ref[0, :, _cols(P_XF)] = xf_bf16
    zp_ref[0, :, _cols(P_GATE)] = proj(xb, 7).astype(BF16)
    a, b = _lru_coeffs(xf, xf_bf16, wg_ref, bg_ref, lam_ref)
    hb, carry = _linear_scan(a, b, hc_ref[...], reverse=True)
    hc_ref[...] = carry
    zp_ref[0, :, _cols(P_HB)] = hb.astype(BF16)

    pbuf_ref[0:HALO] = proj(xpb, 9) * proj(xpb, 10) * has_prev
    pbuf_ref[HALO:HALO + n] = proj(xb, 9) * proj(xb, 10)
    pbuf_ref[HALO + n:HALO + n + HALO] = proj(xnb, 9) * proj(xnb, 10) * has_next
    conv = scw_ref[0:1, :] * pbuf_ref[pl.ds(HALO - 1, n), :]
    for j in range(1, 3):
        conv = conv + scw_ref[j:j + 1, :] * pbuf_ref[pl.ds(HALO + j - 1, n), :]
    zp_ref[0, :, _cols(P_YD)] = (proj(xb, 8) * conv).astype(BF16)


def _mix_bwd(x, cos, sin, w, lng, lnb, ws, bs, cw, cb, wg, bg, lam, scw, kdec, qdec, dm, bd):
    bsz, s, d = x.shape
    nt = s // TILE
    hb = TILE // HALO
    half = GROUP_WIDTH // 2

    def full(a):
        return pl.BlockSpec(a.shape, lambda b, t: (0,) * a.ndim)

    def rev(b, t):
        return (b, nt - 1 - t, 0)

    in_specs = [
        pl.BlockSpec((1, TILE, d), rev),
        pl.BlockSpec((1, HALO, d), lambda b, t: (b, jnp.maximum((nt - 1 - t) * hb - 1, 0), 0)),
        pl.BlockSpec((1, HALO, d), lambda b, t: (b, jnp.minimum((nt - t) * hb, nt * hb - 1), 0)),
        pl.BlockSpec((1, TILE, half), rev),
        pl.BlockSpec((1, TILE, half), rev),
    ] + [full(a) for a in (w, lng, lnb, ws, bs, cw, cb, wg, bg, lam, scw, kdec, qdec, dm, bd)]
    return pl.pallas_call(
        _mix_bwd_kernel,
        grid=(bsz, nt),
        in_specs=in_specs,
        out_specs=pl.BlockSpec((1, TILE, PACK_COLS), rev),
        out_shape=jax.ShapeDtypeStruct((bsz, s, PACK_COLS), BF16),
        scratch_shapes=[
            pltpu.VMEM((GROUP_WIDTH, GROUP_WIDTH), F32),
            pltpu.VMEM((1, GROUP_WIDTH), F32),
            pltpu.VMEM((TILE + 2 * HALO, GROUP_WIDTH), F32),
            pltpu.VMEM((TILE + 2 * HALO, GROUP_WIDTH), F32),
        ],
        compiler_params=pltpu.CompilerParams(
            dimension_semantics=("arbitrary", "arbitrary"),
            vmem_limit_bytes=VMEM_LIMIT_BYTES),
        name="mix_bwd",
    )(x, x, x, cos, sin, w, lng, lnb, ws, bs, cw, cb, wg, bg, lam, scw, kdec, qdec, dm, bd)


def _mix_fwd_kernel(alpha, ff_chunk,
                    x_ref, zp_ref, intra_ref, kdec_ref, qdec_ref, dm_ref, bd_ref, avg_ref,
                    gng_ref, gnb_ref, wg_ref, bg_ref, lam_ref,
                    wo_ref, l1g_ref, l1b_ref, fg_ref, fu_ref, fd_ref, l2g_ref, l2b_ref,
                    o_ref, r_ref, hc_ref):
    t = pl.program_id(1)
    n = x_ref.shape[1]
    nchunk = n // CHUNK

    @pl.when(t == 0)
    def _():
        r_ref[...] = jnp.zeros_like(r_ref)
        hc_ref[...] = jnp.zeros_like(hc_ref)

    nat = _lane_head_masks(permuted=False)
    per = _lane_head_masks(permuted=True)

    yb = []
    for c in range(nchunk):
        rows = slice(c * CHUNK, (c + 1) * CHUNK)
        qc = zp_ref[0, rows, _cols(P_Q)].astype(F32)
        k16 = zp_ref[0, rows, _cols(P_K)]
        v16 = zp_ref[0, rows, _cols(P_V)]
        vc = v16.astype(F32)
        scores = []
        for h in range(HEADS):
            sh = _dot_nt(jnp.where(per[h], qc, 0.0).astype(BF16), k16)
            scores.append(sh * intra_ref[:, h * CHUNK:(h + 1) * CHUNK])
        scores = jnp.concatenate(scores, axis=1).astype(BF16)
        vstack = jnp.concatenate([jnp.where(m, vc, 0.0) for m in nat], axis=0).astype(BF16)
        state = r_ref[...]
        out = (_dot(scores, vstack)
               + _dot((qc * qdec_ref[...]).astype(BF16), state.astype(BF16))
               + zp_ref[0, rows, _cols(P_OB)].astype(F32))
        upd = _dot_tn((k16.astype(F32) * kdec_ref[...]).astype(BF16), v16)
        r_ref[...] = dm_ref[...] * state + bd_ref[...] * upd
        mu = _dot(out.astype(BF16), avg_ref[...])
        dlt = out - mu
        var = _dot((dlt * dlt).astype(BF16), avg_ref[...])
        normed = dlt * lax.rsqrt(var + LN_EPS) * gng_ref[...] + gnb_ref[...]
        g = zp_ref[0, rows, _cols(P_G)].astype(F32)
        yb.append((jax.nn.silu(g) * normed).astype(BF16))
    yb = jnp.concatenate(yb, axis=0)

    xf16 = zp_ref[0, :, _cols(P_XF)]
    a, b = _lru_coeffs(xf16.astype(F32), xf16, wg_ref, bg_ref, lam_ref)
    hf, carry = _linear_scan(a, b, hc_ref[...], reverse=False)
    hc_ref[...] = carry
    gate = zp_ref[0, :, _cols(P_GATE)].astype(F32)
    yc = ((hf + zp_ref[0, :, _cols(P_HB)].astype(F32)) * jax.nn.gelu(gate)).astype(BF16)

    y = jnp.concatenate([zp_ref[0, :, _cols(P_YA)], yb, yc, zp_ref[0, :, _cols(P_YD)]], axis=1)
    x1 = _layer_norm(alpha * x_ref[0] + _dot(y, wo_ref[...]), l1g_ref[...], l1b_ref[...])

    x1b = x1.astype(BF16)
    d_ff = fg_ref.shape[1]
    acc = None
    for j in range(0, d_ff, ff_chunk):
        cs = slice(j, j + ff_chunk)
        hid = (jax.nn.silu(_dot(x1b, fg_ref[:, cs])) * _dot(x1b, fu_ref[:, cs])).astype(BF16)
        part = _dot(hid, fd_ref[cs, :])
        acc = part if acc is None else acc + part
    o_ref[0] = _layer_norm(alpha * x1 + acc, l2g_ref[...], l2b_ref[...])


def _mix_fwd(alpha, x, zp, intra, kdec, qdec, dm, bd, avg, gng, gnb, wg, bg, lam,
             wo, l1g, l1b, fg, fu, fd, l2g, l2b):
    bsz, s, d = x.shape
    nt = s // TILE
    d_ff = fg.shape[1]
    ff_chunk = d_ff // 2

    def full(a):
        return pl.BlockSpec(a.shape, lambda b, t: (0,) * a.ndim, pipeline_mode=pl.Buffered(1))

    consts = (intra, kdec, qdec, dm, bd, avg, gng, gnb, wg, bg, lam, wo, l1g, l1b, fg, fu, fd, l2g, l2b)
    in_specs = [
        pl.BlockSpec((1, TILE, d), lambda b, t: (b, t, 0)),
        pl.BlockSpec((1, TILE, PACK_COLS), lambda b, t: (b, t, 0)),
    ] + [full(a) for a in consts]
    return pl.pallas_call(
        functools.partial(_mix_fwd_kernel, alpha, ff_chunk),
        grid=(bsz, nt),
        in_specs=in_specs,
        out_specs=pl.BlockSpec((1, TILE, d), lambda b, t: (b, t, 0)),
        out_shape=jax.ShapeDtypeStruct((bsz, s, d), F32),
        scratch_shapes=[
            pltpu.VMEM((GROUP_WIDTH, GROUP_WIDTH), F32),
            pltpu.VMEM((1, GROUP_WIDTH), F32),
        ],
        compiler_params=pltpu.CompilerParams(
            dimension_semantics=("arbitrary", "arbitrary"),
            vmem_limit_bytes=VMEM_LIMIT_BYTES),
        name="mix_fwd",
    )(x, zp, *consts)


def _permuted_head_of_lane():
    lane = np.arange(GROUP_WIDTH)
    return (lane % (GROUP_WIDTH // 2)) // HALF_DIM


def _rotary_column_order():
    lane = np.arange(GROUP_WIDTH)
    head = _permuted_head_of_lane()
    second = lane // (GROUP_WIDTH // 2)
    return head * HEAD_DIM + second * HALF_DIM + lane % HALF_DIM


def _retention_tables():
    log_gamma = np.log1p(-np.exp2(-5.0 - np.arange(HEADS, dtype=np.float64)))
    idx = np.arange(CHUNK, dtype=np.float64)
    intra = np.exp(log_gamma[:, None, None] * np.abs(idx[:, None] - idx[None, :]))
    intra = np.concatenate(list(intra), axis=1)
    lg_lane = log_gamma[_permuted_head_of_lane()]
    kdec_f = np.exp(lg_lane[None, :] * (CHUNK - 1 - idx)[:, None])
    kdec_b = np.exp(lg_lane[None, :] * idx[:, None])
    qdec_f = np.exp(lg_lane[None, :] * (idx + 1.0)[:, None])
    qdec_b = np.exp(lg_lane[None, :] * (CHUNK - idx)[:, None])
    same = _permuted_head_of_lane()[:, None] == (np.arange(GROUP_WIDTH) // HEAD_DIM)[None, :]
    dm = np.where(same, np.exp(lg_lane * CHUNK)[:, None], 0.0)
    bd = same.astype(np.float64)
    nat_head = np.arange(GROUP_WIDTH) // HEAD_DIM
    avg = (nat_head[:, None] == nat_head[None, :]) / HEAD_DIM
    f = lambda a: jnp.asarray(a, F32)
    return dict(intra=f(intra), kdec_f=f(kdec_f), kdec_b=f(kdec_b), qdec_f=f(qdec_f), qdec_b=f(qdec_b),
                dm=f(dm), bd=f(bd), avg=jnp.asarray(avg, BF16))


def _block_diag(w):
    eye = jnp.eye(HEADS, dtype=w.dtype)
    return jnp.einsum('hij,hg->higj', w, eye).reshape(GROUP_WIDTH, GROUP_WIDTH)


def kernel(x, positions, w_in, gmlp_ln_g, gmlp_ln_b, gmlp_ws, gmlp_bs, ret_gn_g, ret_gn_b, lru_conv_w, lru_conv_b, lru_wa, lru_ba, lru_wx, lru_bx, lru_lambda, sc_conv_w, w_out, ln1_g, ln1_b, ffn_wg, ffn_wu, ffn_wd, ln2_g, ln2_b):
    depth = w_in.shape[0]
    assert x.shape[1] % TILE == 0 and x.shape[2] % GROUP_WIDTH == 0
    assert w_in.shape[2] == N_SLICES * GROUP_WIDTH
    alpha = (2 * depth) ** 0.25
    tb = _retention_tables()

    inv_freq = ROPE_BASE ** (-np.arange(0, HEAD_DIM, 2, dtype=np.float64) / HEAD_DIM)
    inv_row = jnp.asarray(np.tile(inv_freq, HEADS)[None, :], F32)
    cos, sin = _rope_tables(positions, inv_row)

    order = np.arange(N_SLICES * GROUP_WIDTH)
    rot = _rotary_column_order()
    order[_cols(2)] = 2 * GROUP_WIDTH + rot
    order[_cols(3)] = 3 * GROUP_WIDTH + rot
    row = lambda a: a.reshape(1, -1)

    for l in range(depth):
        w = w_in[l][:, order].astype(BF16)
        ws = jnp.transpose(gmlp_ws[l], (1, 0, 2)).reshape(CHUNK, HEADS * CHUNK).astype(BF16)
        bs = jnp.repeat(gmlp_bs[l].T, HEAD_DIM, axis=1)
        gate_w = [jnp.concatenate([_block_diag(lru_wa[l, z]), _block_diag(lru_wx[l, z])], axis=1).astype(BF16)
                  for z in range(2)]
        gate_b = [jnp.concatenate([lru_ba[l, z], lru_bx[l, z]])[None, :] for z in range(2)]
        zp = _mix_bwd(x, cos, sin, w, row(gmlp_ln_g[l]), row(gmlp_ln_b[l]), ws, bs,
                      lru_conv_w[l], row(lru_conv_b[l]), gate_w[1], gate_b[1], row(lru_lambda[l, 1]),
                      sc_conv_w[l], tb['kdec_b'], tb['qdec_b'], tb['dm'], tb['bd'])
        x = _mix_fwd(alpha, x, zp, tb['intra'], tb['kdec_f'], tb['qdec_f'], tb['dm'], tb['bd'], tb['avg'],
                     row(ret_gn_g[l]), row(ret_gn_b[l]), gate_w[0], gate_b[0], row(lru_lambda[l, 0]),
                     w_out[l].astype(BF16), row(ln1_g[l]), row(ln1_b[l]),
                     ffn_wg[l].astype(BF16), ffn_wu[l].astype(BF16), ffn_wd[l].astype(BF16),
                     row(ln2_g[l]), row(ln2_b[l]))
    return x
```

```python
import functools
import math

import numpy as np
import jax
import jax.numpy as jnp
from jax import lax
from jax.experimental import pallas as pl
from jax.experimental.pallas import tpu as pltpu

F32 = jnp.float32
BF16 = jnp.bfloat16

GROUP_WIDTH = 256
HEADS = 4
HEAD_DIM = 64
HALF_DIM = HEAD_DIM // 2
N_SLICES = 11
CHUNK = 128
ROPE_BASE = 10000.0
LRU_C = 8.0
LN_EPS = 1e-5

SUBLANES = 8
MXU_WIDTH = 256
TILE = 512
HALO = 16
VMEM_LIMIT_BYTES = 56 * 1024 * 1024

P_YA, P_Q, P_K, P_V, P_G, P_XF, P_GATE, P_HB, P_OB, P_YD = range(10)
PACK_COLS = 10 * GROUP_WIDTH


def _cols(i, n=1):
    return slice(i * GROUP_WIDTH, (i + n) * GROUP_WIDTH)


def _dot(a, b):
    return jnp.dot(a, b, preferred_element_type=F32)


def _dot_tn(a, b):
    return lax.dot_general(a, b, (((0,), (0,)), ((), ())), preferred_element_type=F32)


def _dot_nt(a, b):
    return lax.dot_general(a, b, (((1,), (1,)), ((), ())), preferred_element_type=F32)


def _layer_norm(x, g, b):
    mu = jnp.mean(x, axis=-1, keepdims=True)
    d = x - mu
    var = jnp.mean(d * d, axis=-1, keepdims=True)
    return d * lax.rsqrt(var + LN_EPS) * g + b


def _softplus(x):
    return jnp.maximum(x, 0.0) + jnp.log1p(jnp.exp(-jnp.abs(x)))


def _lane_head_masks(permuted):
    lane = lax.broadcasted_iota(jnp.int32, (1, GROUP_WIDTH), 1)
    if permuted:
        head = (lane & (GROUP_WIDTH // 2 - 1)) >> 5
    else:
        head = lane >> 6
    return [head == h for h in range(HEADS)]


def _linear_scan(a, b, carry, reverse):
    n, w = a.shape
    nblk = n // SUBLANES
    a3 = a.reshape(nblk, SUBLANES, w)
    b3 = b.reshape(nblk, SUBLANES, w)
    sub = lax.broadcasted_iota(jnp.int32, (nblk, SUBLANES, w), 1)
    s = 1
    while s < SUBLANES:
        if reverse:
            a_s = pltpu.roll(a3, SUBLANES - s, 1)
            b_s = pltpu.roll(b3, SUBLANES - s, 1)
            valid = sub < SUBLANES - s
        else:
            a_s = pltpu.roll(a3, s, 1)
            b_s = pltpu.roll(b3, s, 1)
            valid = sub >= s
        b3 = b3 + jnp.where(valid, a3 * b_s, 0.0)
        a3 = jnp.where(valid, a3 * a_s, a3)
        s *= 2
    out = [None] * nblk
    order = range(nblk - 1, -1, -1) if reverse else range(nblk)
    for i in order:
        h = a3[i] * carry + b3[i]
        carry = h[0:1] if reverse else h[SUBLANES - 1:SUBLANES]
        out[i] = h
    return jnp.concatenate(out, axis=0), carry


def _lru_coeffs(xf, xf_bf16, wg_ref, bg_ref, lam_ref):
    gates = _dot(xf_bf16, wg_ref[...]) + bg_ref[...]
    r = jax.nn.sigmoid(gates[:, :GROUP_WIDTH])
    i = jax.nn.sigmoid(gates[:, GROUP_WIDTH:])
    log_a = (-LRU_C * _softplus(-lam_ref[...])) * r
    a = jnp.exp(log_a)
    b = xf * i * jnp.sqrt(1.0 - a * a)
    return a, b


def _merge_evenly(first, second):
    if len(first) < len(second):
        first, second = second, first
    out, done = [], 0
    for i, f in enumerate(first):
        out.append(f)
        want = (i + 1) * len(second) // len(first)
        out.extend(second[done:want])
        done = want
    return out


def _merge_lookahead(producers, consumers):
    out, emitted = [], 0
    for i, (needs, run) in enumerate(consumers):
        want = max(max(needs, default=-1) + 1, -(-(i + 1) * len(producers) // len(consumers)))
        while emitted < min(want, len(producers)):
            out.append(producers[emitted])
            emitted += 1
        out.append(run)
    return out + producers[emitted:]


def _rope_kernel(pos_ref, inv_ref, cos_ref, sin_ref):
    ang = pos_ref[0] * inv_ref[...]
    cos_ref[0] = jnp.cos(ang)
    sin_ref[0] = jnp.sin(ang)


def _rope_tables(positions, inv_row):
    bsz, s = positions.shape
    half = GROUP_WIDTH // 2
    pos_b = jnp.broadcast_to(positions.astype(F32)[..., None], (bsz, s, half))
    spec = pl.BlockSpec((1, TILE, half), lambda b, t: (b, t, 0))
    return pl.pallas_call(
        _rope_kernel,
        grid=(bsz, s // TILE),
        in_specs=[spec, pl.BlockSpec((1, half), lambda b, t: (0, 0))],
        out_specs=[spec, spec],
        out_shape=[jax.ShapeDtypeStruct((bsz, s, half), F32)] * 2,
        name="rope_tables",
    )(pos_b, inv_row)


def _mix_bwd_kernel(x_ref, xp_ref, xn_ref, cos_ref, sin_ref, w_ref,
                    lng_ref, lnb_ref, ws_ref, bs_ref,
                    cw_ref, cb_ref, wg_ref, bg_ref, lam_ref, scw_ref,
                    kdec_ref, qdec_ref, dm_ref, bd_ref,
                    zp_ref, l_ref, hc_ref, cbuf_ref, pbuf_ref):
    t = pl.program_id(1)
    nt = pl.num_programs(1)
    n = x_ref.shape[1]
    nchunk = n // CHUNK
    half = GROUP_WIDTH // 2

    @pl.when(t == 0)
    def _():
        l_ref[...] = jnp.zeros_like(l_ref)
        hc_ref[...] = jnp.zeros_like(hc_ref)

    has_prev = jnp.where(t != nt - 1, 1.0, 0.0).astype(F32)
    has_next = jnp.where(t != 0, 1.0, 0.0).astype(F32)

    xb = x_ref[0].astype(BF16)
    xpb = xp_ref[0].astype(BF16)
    xnb = xn_ref[0].astype(BF16)
    nat = _lane_head_masks(permuted=False)
    z, zprev, znext = {}, {}, {}
    st = {"carry": None}

    proj_order = (6, 7, 0, 1, 2, 3, 4, 5, 9, 10, 8)
    place = {g: i for i, g in enumerate(proj_order)}

    def project(g):
        def run():
            z[g] = _dot(xb, w_ref[:, _cols(g)])
            if g in (6, 9, 10):
                zprev[g] = _dot(xpb, w_ref[:, _cols(g)]) * has_prev
                znext[g] = _dot(xnb, w_ref[:, _cols(g)]) * has_next
        return run

    producers = [project(g) for g in proj_order]
    consumers = []

    def piece(needs):
        def register(run):
            consumers.append(([place[g] for g in needs], run))
            return run
        return register

    @piece((6,))
    def _():
        cbuf_ref[0:HALO] = zprev[6]
        cbuf_ref[HALO:HALO + n] = z[6]
        cbuf_ref[HALO + n:HALO + n + HALO] = znext[6]
        xf = cb_ref[...]
        for j in range(4):
            xf = xf + cw_ref[j:j + 1, :] * cbuf_ref[pl.ds(HALO + j - 2, n), :]
        st["xf"] = xf
        st["xf16"] = xf.astype(BF16)
        zp_ref[0, :, _cols(P_XF)] = st["xf16"]

    for c in range(nchunk - 1, -1, -1):
        rows = slice(c * CHUNK, (c + 1) * CHUNK)

        @piece(())
        def _(rows=rows):
            st["ab"] = _lru_coeffs(st["xf"][rows], st["xf16"][rows], wg_ref, bg_ref, lam_ref)

        @piece(())
        def _(rows=rows, c=c):
            carry = hc_ref[...] if st["carry"] is None else st["carry"]
            hb, st["carry"] = _linear_scan(*st["ab"], carry, reverse=True)
            if c == 0:
                hc_ref[...] = st["carry"]
            zp_ref[0, rows, _cols(P_HB)] = hb.astype(BF16)

    @piece((7,))
    def _():
        zp_ref[0, :, _cols(P_GATE)] = z[7].astype(BF16)

    @piece((0,))
    def _():
        st["u"] = jax.nn.gelu(z[0])

    @piece((1,))
    def _():
        st["v"] = _layer_norm(jax.nn.gelu(z[1]), lng_ref[...], lnb_ref[...])

    for c in range(nchunk):
        rows = slice(c * CHUNK, (c + 1) * CHUNK)

        @piece(())
        def _(rows=rows):
            vc = st["v"][rows]
            vstack = jnp.concatenate([jnp.where(m, vc, 0.0) for m in nat], axis=0).astype(BF16)
            mixed = _dot(ws_ref[...], vstack) + bs_ref[...]
            zp_ref[0, rows, _cols(P_YA)] = (st["u"][rows] * mixed).astype(BF16)

    def rotary(v):
        cos = cos_ref[0]
        sin = sin_ref[0]
        v1, v2 = v[:, :half], v[:, half:]
        return jnp.concatenate([v1 * cos - v2 * sin, v2 * cos + v1 * sin], axis=1)

    @piece((2,))
    def _():
        st["q"] = rotary(z[2])
        zp_ref[0, :, _cols(P_Q)] = st["q"].astype(BF16)

    @piece((3,))
    def _():
        st["k"] = rotary(z[3]) * HEAD_DIM ** -0.5
        zp_ref[0, :, _cols(P_K)] = st["k"].astype(BF16)

    @piece((4, 5))
    def _():
        st["vb"] = z[4].astype(BF16)
        zp_ref[0, :, _cols(P_V)] = st["vb"]
        zp_ref[0, :, _cols(P_G)] = z[5].astype(BF16)

    for c in range(nchunk - 1, -1, -1):
        rows = slice(c * CHUNK, (c + 1) * CHUNK)

        @piece(())
        def _(rows=rows):
            state = l_ref[...]
            ob = _dot((st["q"][rows] * qdec_ref[...]).astype(BF16), state.astype(BF16))
            zp_ref[0, rows, _cols(P_OB)] = ob.astype(BF16)
            upd = _dot_tn((st["k"][rows] * kdec_ref[...]).astype(BF16), st["vb"][rows])
            l_ref[...] = dm_ref[...] * state + bd_ref[...] * upd

    @piece((9, 10))
    def _():
        pbuf_ref[0:HALO] = zprev[9] * zprev[10]
        pbuf_ref[HALO:HALO + n] = z[9] * z[10]
        pbuf_ref[HALO + n:HALO + n + HALO] = znext[9] * znext[10]

    @piece((8,))
    def _():
        conv = scw_ref[0:1, :] * pbuf_ref[pl.ds(HALO - 1, n), :]
        for j in range(1, 3):
            conv = conv + scw_ref[j:j + 1, :] * pbuf_ref[pl.ds(HALO + j - 1, n), :]
        zp_ref[0, :, _cols(P_YD)] = (z[8] * conv).astype(BF16)

    for run in _merge_lookahead(producers, consumers):
        run()


def _mix_bwd(x, cos, sin, w, lng, lnb, ws, bs, cw, cb, wg, bg, lam, scw, kdec, qdec, dm, bd):
    bsz, s, d = x.shape
    nt = s // TILE
    hb = TILE // HALO
    half = GROUP_WIDTH // 2

    def full(a):
        return pl.BlockSpec(a.shape, lambda b, t: (0,) * a.ndim)

    def rev(b, t):
        return (b, nt - 1 - t, 0)

    in_specs = [
        pl.BlockSpec((1, TILE, d), rev),
        pl.BlockSpec((1, HALO, d), lambda b, t: (b, jnp.maximum((nt - 1 - t) * hb - 1, 0), 0)),
        pl.BlockSpec((1, HALO, d), lambda b, t: (b, jnp.minimum((nt - t) * hb, nt * hb - 1), 0)),
        pl.BlockSpec((1, TILE, half), rev),
        pl.BlockSpec((1, TILE, half), rev),
    ] + [full(a) for a in (w, lng, lnb, ws, bs, cw, cb, wg, bg, lam, scw, kdec, qdec, dm, bd)]
    return pl.pallas_call(
        _mix_bwd_kernel,
        grid=(bsz, nt),
        in_specs=in_specs,
        out_specs=pl.BlockSpec((1, TILE, PACK_COLS), rev),
        out_shape=jax.ShapeDtypeStruct((bsz, s, PACK_COLS), BF16),
        scratch_shapes=[
            pltpu.VMEM((GROUP_WIDTH, GROUP_WIDTH), F32),
            pltpu.VMEM((1, GROUP_WIDTH), F32),
            pltpu.VMEM((TILE + 2 * HALO, GROUP_WIDTH), F32),
            pltpu.VMEM((TILE + 2 * HALO, GROUP_WIDTH), F32),
        ],
        compiler_params=pltpu.CompilerParams(
            dimension_semantics=("arbitrary", "arbitrary"),
            vmem_limit_bytes=VMEM_LIMIT_BYTES),
        name="mix_bwd",
    )(x, x, x, cos, sin, w, lng, lnb, ws, bs, cw, cb, wg, bg, lam, scw, kdec, qdec, dm, bd)


def _mix_fwd_kernel(alpha, ff_width, nt,
                    x_ref, zp_ref, intra_ref, kdec_ref, qdec_ref, dm_ref, bd_ref, avg_ref,
                    gng_ref, gnb_ref, wg_ref, bg_ref, lam_ref,
                    wo_ref, l1g_ref, l1b_ref, fg_ref, fu_ref, fd_ref, l2g_ref, l2b_ref,
                    o_ref, r_ref, hc_ref, x1_ref):
    s = pl.program_id(0)
    n_tiles = pl.num_programs(0) - 1
    n = x_ref.shape[1]
    nchunk = n // CHUNK
    t_in = lax.rem(jnp.minimum(s, n_tiles - 1), nt)
    slot = lax.rem(s, 2)

    @pl.when(s == 0)
    def _():
        x1_ref[...] = jnp.zeros_like(x1_ref)

    @pl.when(t_in == 0)
    def _():
        r_ref[...] = jnp.zeros_like(r_ref)
        hc_ref[...] = jnp.zeros_like(hc_ref)

    ffn = []
    st = {"acc": None}
    x1b = x1_ref[1 - slot].astype(BF16)
    d_ff = fg_ref.shape[1]

    def ffn_up(cs):
        def run():
            st["g"] = _dot(x1b, fg_ref[:, cs])
            st["u"] = _dot(x1b, fu_ref[:, cs])
        return run

    def ffn_down(cs):
        def run():
            hid = (jax.nn.silu(st["g"]) * st["u"]).astype(BF16)
            part = _dot(hid, fd_ref[cs, :])
            st["acc"] = part if st["acc"] is None else st["acc"] + part
        return run

    for j in range(0, d_ff, ff_width):
        cs = slice(j, min(j + ff_width, d_ff))
        ffn += [ffn_up(cs), ffn_down(cs)]

    def ffn_norm():
        o_ref[0] = _layer_norm(alpha * x1_ref[1 - slot] + st["acc"], l2g_ref[...], l2b_ref[...])

    nat = _lane_head_masks(permuted=False)
    per = _lane_head_masks(permuted=True)
    mix = []
    ms = {"carry": None, "yb": [None] * nchunk, "yc": [None] * nchunk}

    def ret_scores(c):
        def run():
            rows = slice(c * CHUNK, (c + 1) * CHUNK)
            qc = zp_ref[0, rows, _cols(P_Q)].astype(F32)
            k16 = zp_ref[0, rows, _cols(P_K)]
            scores = []
            for h in range(HEADS):
                sh = _dot_nt(jnp.where(per[h], qc, 0.0).astype(BF16), k16)
                scores.append(sh * intra_ref[:, h * CHUNK:(h + 1) * CHUNK])
            ms["scores"] = jnp.concatenate(scores, axis=1).astype(BF16)
            ms["qc"] = qc
        return run

    def ret_out(c):
        def run():
            rows = slice(c * CHUNK, (c + 1) * CHUNK)
            k16 = zp_ref[0, rows, _cols(P_K)]
            v16 = zp_ref[0, rows, _cols(P_V)]
            vc = v16.astype(F32)
            vstack = jnp.concatenate([jnp.where(m, vc, 0.0) for m in nat], axis=0).astype(BF16)
            state = r_ref[...]
            ms["out"] = (_dot(ms["scores"], vstack)
                         + _dot((ms["qc"] * qdec_ref[...]).astype(BF16), state.astype(BF16))
                         + zp_ref[0, rows, _cols(P_OB)].astype(F32))
            upd = _dot_tn((k16.astype(F32) * kdec_ref[...]).astype(BF16), v16)
            r_ref[...] = dm_ref[...] * state + bd_ref[...] * upd
        return run

    def ret_norm(c):
        def run():
            rows = slice(c * CHUNK, (c + 1) * CHUNK)
            out = ms["out"]
            mu = _dot(out.astype(BF16), avg_ref[...])
            dlt = out - mu
            var = _dot((dlt * dlt).astype(BF16), avg_ref[...])
            normed = dlt * lax.rsqrt(var + LN_EPS) * gng_ref[...] + gnb_ref[...]
            g = zp_ref[0, rows, _cols(P_G)].astype(F32)
            ms["yb"][c] = (jax.nn.silu(g) * normed).astype(BF16)
        return run

    def lru_coeffs(c):
        def run():
            rows = slice(c * CHUNK, (c + 1) * CHUNK)
            xf16 = zp_ref[0, rows, _cols(P_XF)]
            ms["ab"] = _lru_coeffs(xf16.astype(F32), xf16, wg_ref, bg_ref, lam_ref)
        return run

    def lru_scan(c):
        def run():
            rows = slice(c * CHUNK, (c + 1) * CHUNK)
            carry = hc_ref[...] if ms["carry"] is None else ms["carry"]
            hf, ms["carry"] = _linear_scan(*ms["ab"], carry, reverse=False)
            if c == nchunk - 1:
                hc_ref[...] = ms["carry"]
            gate = zp_ref[0, rows, _cols(P_GATE)].astype(F32)
            hb = zp_ref[0, rows, _cols(P_HB)].astype(F32)
            ms["yc"][c] = ((hf + hb) * jax.nn.gelu(gate)).astype(BF16)
        return run

    def out_proj(c0, c1):
        def run():
            rows = slice(c0 * CHUNK, c1 * CHUNK)
            y = jnp.concatenate([zp_ref[0, rows, _cols(P_YA)],
                                 jnp.concatenate(ms["yb"][c0:c1], axis=0),
                                 jnp.concatenate(ms["yc"][c0:c1], axis=0),
                                 zp_ref[0, rows, _cols(P_YD)]], axis=1)
            x1_ref[slot, rows, :] = _layer_norm(alpha * x_ref[0, rows, :] + _dot(y, wo_ref[...]),
                                                l1g_ref[...], l1b_ref[...])
        return run

    per_proj = 2 if nchunk % 2 == 0 else 1
    for c in range(nchunk):
        mix += [ret_scores(c), ret_out(c), ret_norm(c), lru_coeffs(c), lru_scan(c)]
        if (c + 1) % per_proj == 0:
            mix.append(out_proj(c + 1 - per_proj, c + 1))

    split = len(ffn) * 3 // 4
    for piece in _merge_evenly(ffn[:split], mix) + ffn[split:] + [ffn_norm]:
        piece()


def _mix_fwd(alpha, x, zp, intra, kdec, qdec, dm, bd, avg, gng, gnb, wg, bg, lam,
             wo, l1g, l1b, fg, fu, fd, l2g, l2b):
    bsz, s, d = x.shape
    nt = s // TILE
    n_tiles = bsz * nt

    def full(a):
        return pl.BlockSpec(a.shape, lambda i: (0,) * a.ndim, pipeline_mode=pl.Buffered(1))

    def tile_in(i):
        j = jnp.minimum(i, n_tiles - 1)
        return (j // nt, j % nt, 0)

    def tile_out(i):
        j = jnp.maximum(i - 1, 0)
        return (j // nt, j % nt, 0)

    consts = (intra, kdec, qdec, dm, bd, avg, gng, gnb, wg, bg, lam, wo, l1g, l1b, fg, fu, fd, l2g, l2b)
    in_specs = [
        pl.BlockSpec((1, TILE, d), tile_in),
        pl.BlockSpec((1, TILE, PACK_COLS), tile_in),
    ] + [full(a) for a in consts]
    return pl.pallas_call(
        functools.partial(_mix_fwd_kernel, alpha, MXU_WIDTH, nt),
        grid=(n_tiles + 1,),
        in_specs=in_specs,
        out_specs=pl.BlockSpec((1, TILE, d), tile_out),
        out_shape=jax.ShapeDtypeStruct((bsz, s, d), F32),
        scratch_shapes=[
            pltpu.VMEM((GROUP_WIDTH, GROUP_WIDTH), F32),
            pltpu.VMEM((1, GROUP_WIDTH), F32),
            pltpu.VMEM((2, TILE, d), F32),
        ],
        compiler_params=pltpu.CompilerParams(
            dimension_semantics=("arbitrary",),
            vmem_limit_bytes=VMEM_LIMIT_BYTES),
        name="mix_fwd",
    )(x, zp, *consts)


def _permuted_head_of_lane():
    lane = np.arange(GROUP_WIDTH)
    return (lane % (GROUP_WIDTH // 2)) // HALF_DIM


def _rotary_column_order():
    lane = np.arange(GROUP_WIDTH)
    head = _permuted_head_of_lane()
    second = lane // (GROUP_WIDTH // 2)
    return head * HEAD_DIM + second * HALF_DIM + lane % HALF_DIM


def _retention_tables():
    log_gamma = np.log1p(-np.exp2(-5.0 - np.arange(HEADS, dtype=np.float64)))
    idx = np.arange(CHUNK, dtype=np.float64)
    intra = np.exp(log_gamma[:, None, None] * np.abs(idx[:, None] - idx[None, :]))
    intra = np.concatenate(list(intra), axis=1)
    lg_lane = log_gamma[_permuted_head_of_lane()]
    kdec_f = np.exp(lg_lane[None, :] * (CHUNK - 1 - idx)[:, None])
    kdec_b = np.exp(lg_lane[None, :] * idx[:, None])
    qdec_f = np.exp(lg_lane[None, :] * (idx + 1.0)[:, None])
    qdec_b = np.exp(lg_lane[None, :] * (CHUNK - idx)[:, None])
    same = _permuted_head_of_lane()[:, None] == (np.arange(GROUP_WIDTH) // HEAD_DIM)[None, :]
    dm = np.where(same, np.exp(lg_lane * CHUNK)[:, None], 0.0)
    bd = same.astype(np.float64)
    nat_head = np.arange(GROUP_WIDTH) // HEAD_DIM
    avg = (nat_head[:, None] == nat_head[None, :]) / HEAD_DIM
    f = lambda a: jnp.asarray(a, F32)
    return dict(intra=f(intra), kdec_f=f(kdec_f), kdec_b=f(kdec_b), qdec_f=f(qdec_f), qdec_b=f(qdec_b),
                dm=f(dm), bd=f(bd), avg=jnp.asarray(avg, BF16))


def _block_diag(w):
    eye = jnp.eye(HEADS, dtype=w.dtype)
    return jnp.einsum('hij,hg->higj', w, eye).reshape(GROUP_WIDTH, GROUP_WIDTH)


def kernel(x, positions, w_in, gmlp_ln_g, gmlp_ln_b, gmlp_ws, gmlp_bs, ret_gn_g, ret_gn_b, lru_conv_w, lru_conv_b, lru_wa, lru_ba, lru_wx, lru_bx, lru_lambda, sc_conv_w, w_out, ln1_g, ln1_b, ffn_wg, ffn_wu, ffn_wd, ln2_g, ln2_b):
    depth = w_in.shape[0]
    assert x.shape[1] % TILE == 0 and x.shape[2] % GROUP_WIDTH == 0
    assert w_in.shape[2] == N_SLICES * GROUP_WIDTH
    alpha = (2 * depth) ** 0.25
    tb = _retention_tables()

    inv_freq = ROPE_BASE ** (-np.arange(0, HEAD_DIM, 2, dtype=np.float64) / HEAD_DIM)
    inv_row = jnp.asarray(np.tile(inv_freq, HEADS)[None, :], F32)
    cos, sin = _rope_tables(positions, inv_row)

    order = np.arange(N_SLICES * GROUP_WIDTH)
    rot = _rotary_column_order()
    order[_cols(2)] = 2 * GROUP_WIDTH + rot
    order[_cols(3)] = 3 * GROUP_WIDTH + rot
    row = lambda a: a.reshape(1, -1)

    for l in range(depth):
        w = w_in[l][:, order].astype(BF16)
        ws = jnp.transpose(gmlp_ws[l], (1, 0, 2)).reshape(CHUNK, HEADS * CHUNK).astype(BF16)
        bs = jnp.repeat(gmlp_bs[l].T, HEAD_DIM, axis=1)
        gate_w = [jnp.concatenate([_block_diag(lru_wa[l, z]), _block_diag(lru_wx[l, z])], axis=1).astype(BF16)
                  for z in range(2)]
        gate_b = [jnp.concatenate([lru_ba[l, z], lru_bx[l, z]])[None, :] for z in range(2)]
        zp = _mix_bwd(x, cos, sin, w, row(gmlp_ln_g[l]), row(gmlp_ln_b[l]), ws, bs,
                      lru_conv_w[l], row(lru_conv_b[l]), gate_w[1], gate_b[1], row(lru_lambda[l, 1]),
                      sc_conv_w[l], tb['kdec_b'], tb['qdec_b'], tb['dm'], tb['bd'])
        x = _mix_fwd(alpha, x, zp, tb['intra'], tb['kdec_f'], tb['qdec_f'], tb['dm'], tb['bd'], tb['avg'],
                     row(ret_gn_g[l]), row(ret_gn_b[l]), gate_w[0], gate_b[0], row(lru_lambda[l, 0]),
                     w_out[l].astype(BF16), row(ln1_g[l]), row(ln1_b[l]),
                     ffn_wg[l].astype(BF16), ffn_wu[l].astype(BF16), ffn_wd[l].astype(BF16),
                     row(ln2_g[l]), row(ln2_b[l]))
    return x
```

```python
import functools
import math

import numpy as np
import jax
import jax.numpy as jnp
from jax import lax
from jax.experimental import pallas as pl
from jax.experimental.pallas import tpu as pltpu

F32 = jnp.float32
BF16 = jnp.bfloat16

GROUP_WIDTH = 256
HEADS = 4
HEAD_DIM = 64
HALF_DIM = HEAD_DIM // 2
N_SLICES = 11
CHUNK = 128
ROPE_BASE = 10000.0
LRU_C = 8.0
LN_EPS = 1e-5

SUBLANES = 8
MXU_WIDTH = 256
PROJ_ROWS = 256
CONV_SLICES = (6, 9, 10)
TILE = 512
HALO = 16
VMEM_LIMIT_BYTES = 56 * 1024 * 1024

P_YA, P_Q, P_K, P_V, P_G, P_XF, P_GATE, P_HB, P_OB, P_YD = range(10)
PACK_COLS = 10 * GROUP_WIDTH


def _cols(i, n=1):
    return slice(i * GROUP_WIDTH, (i + n) * GROUP_WIDTH)


def _dot(a, b):
    return jnp.dot(a, b, preferred_element_type=F32)


def _dot_tn(a, b):
    return lax.dot_general(a, b, (((0,), (0,)), ((), ())), preferred_element_type=F32)


def _dot_nt(a, b):
    return lax.dot_general(a, b, (((1,), (1,)), ((), ())), preferred_element_type=F32)


def _layer_norm(x, g, b):
    mu = jnp.mean(x, axis=-1, keepdims=True)
    d = x - mu
    var = jnp.mean(d * d, axis=-1, keepdims=True)
    return d * lax.rsqrt(var + LN_EPS) * g + b


def _softplus(x):
    return jnp.maximum(x, 0.0) + jnp.log1p(jnp.exp(-jnp.abs(x)))


def _lane_head_masks(permuted):
    lane = lax.broadcasted_iota(jnp.int32, (1, GROUP_WIDTH), 1)
    if permuted:
        head = (lane & (GROUP_WIDTH // 2 - 1)) >> 5
    else:
        head = lane >> 6
    return [head == h for h in range(HEADS)]


def _linear_scan(a, b, carry, reverse):
    n, w = a.shape
    nblk = n // SUBLANES
    a3 = a.reshape(nblk, SUBLANES, w)
    b3 = b.reshape(nblk, SUBLANES, w)
    sub = lax.broadcasted_iota(jnp.int32, (nblk, SUBLANES, w), 1)
    s = 1
    while s < SUBLANES:
        if reverse:
            a_s = pltpu.roll(a3, SUBLANES - s, 1)
            b_s = pltpu.roll(b3, SUBLANES - s, 1)
            valid = sub < SUBLANES - s
        else:
            a_s = pltpu.roll(a3, s, 1)
            b_s = pltpu.roll(b3, s, 1)
            valid = sub >= s
        b3 = b3 + jnp.where(valid, a3 * b_s, 0.0)
        a3 = jnp.where(valid, a3 * a_s, a3)
        s *= 2
    out = [None] * nblk
    order = range(nblk - 1, -1, -1) if reverse else range(nblk)
    for i in order:
        h = a3[i] * carry + b3[i]
        carry = h[0:1] if reverse else h[SUBLANES - 1:SUBLANES]
        out[i] = h
    return jnp.concatenate(out, axis=0), carry


def _lru_coeffs(xf, xf_bf16, wg_ref, bg_ref, lam_ref):
    gates = _dot(xf_bf16, wg_ref[...]) + bg_ref[...]
    r = jax.nn.sigmoid(gates[:, :GROUP_WIDTH])
    i = jax.nn.sigmoid(gates[:, GROUP_WIDTH:])
    log_a = (-LRU_C * _softplus(-lam_ref[...])) * r
    a = jnp.exp(log_a)
    b = xf * i * jnp.sqrt(1.0 - a * a)
    return a, b


def _merge_evenly(first, second):
    if len(first) < len(second):
        first, second = second, first
    out, done = [], 0
    for i, f in enumerate(first):
        out.append(f)
        want = (i + 1) * len(second) // len(first)
        out.extend(second[done:want])
        done = want
    return out


def _merge_by_weight(producers, consumers):
    total = sum(w for w, _ in consumers)
    out, emitted, done = [], 0, 0.0
    for weight, run in consumers:
        want = int(done / total * len(producers)) + 1
        while emitted < min(want, len(producers)):
            out.append(producers[emitted])
            emitted += 1
        out.append(run)
        done += weight
    return out + producers[emitted:]


def _rope_kernel(pos_ref, inv_ref, cos_ref, sin_ref):
    ang = pos_ref[0] * inv_ref[...]
    cos_ref[0] = jnp.cos(ang)
    sin_ref[0] = jnp.sin(ang)


def _rope_tables(positions, inv_row):
    bsz, s = positions.shape
    half = GROUP_WIDTH // 2
    packed = half // HALF_DIM
    rows = s // packed
    pos_b = jnp.repeat(positions.astype(F32).reshape(bsz, rows, packed), HALF_DIM, axis=2)
    blk = min(TILE, rows)
    spec = pl.BlockSpec((1, blk, half), lambda b, t: (b, t, 0))
    cos, sin = pl.pallas_call(
        _rope_kernel,
        grid=(bsz, rows // blk),
        in_specs=[spec, pl.BlockSpec((1, half), lambda b, t: (0, 0))],
        out_specs=[spec, spec],
        out_shape=[jax.ShapeDtypeStruct((bsz, rows, half), F32)] * 2,
        name="rope_tables",
    )(pos_b, inv_row)
    spread = lambda a: jnp.tile(a.reshape(bsz, s, HALF_DIM), (1, 1, packed))
    return spread(cos), spread(sin)


def _mix_bwd_kernel(nt, x_ref, xp_ref, xn_ref, cos_ref, sin_ref, w_ref,
                    lng_ref, lnb_ref, ws_ref, bs_ref,
                    cw_ref, cb_ref, wg_ref, bg_ref, lam_ref, scw_ref,
                    kdec_ref, qdec_ref, dm_ref, bd_ref,
                    zp_ref, l_ref, hc_ref, pbuf_ref, za_ref, zb_ref):
    s = pl.program_id(0)
    n_tiles = pl.num_programs(0) - 1
    n = x_ref.shape[1]
    nchunk = n // CHUNK
    half = GROUP_WIDTH // 2
    seq_tile = nt - 1 - lax.rem(jnp.minimum(s, n_tiles - 1), nt)

    @pl.when(s == 0)
    def _():
        zb_ref[...] = jnp.zeros_like(zb_ref)

    @pl.when(lax.rem(jnp.maximum(s - 1, 0), nt) == 0)
    def _():
        l_ref[...] = jnp.zeros_like(l_ref)
        hc_ref[...] = jnp.zeros_like(hc_ref)

    def step(zw_ref, zr_ref):
        has_prev = jnp.where(seq_tile != 0, 1.0, 0.0).astype(F32)
        has_next = jnp.where(seq_tile != nt - 1, 1.0, 0.0).astype(F32)
        xb = x_ref[0].astype(BF16)
        producers = []

        def project(g, r):
            def run():
                zw_ref[HALO + r:HALO + r + PROJ_ROWS, _cols(g)] = _dot(xb[r:r + PROJ_ROWS], w_ref[:, _cols(g)])
            return run

        def project_halo():
            xpb = xp_ref[0].astype(BF16)
            xnb = xn_ref[0].astype(BF16)
            for g in CONV_SLICES:
                zw_ref[0:HALO, _cols(g)] = _dot(xpb, w_ref[:, _cols(g)]) * has_prev
                zw_ref[HALO + n:HALO + n + HALO, _cols(g)] = _dot(xnb, w_ref[:, _cols(g)]) * has_next

        for g in range(N_SLICES):
            producers += [project(g, r) for r in range(0, n, PROJ_ROWS)]
        producers.append(project_halo)

        nat = _lane_head_masks(permuted=False)
        consumers = []
        st = {"carry": None}

        def z(g, c, shift=0):
            return zr_ref[pl.ds(HALO + c * CHUNK + shift, CHUNK), _cols(g)]

        def piece(weight):
            def register(run):
                consumers.append((weight, run))
                return run
            return register

        def rotary(v, rows):
            cos = cos_ref[0, rows, :]
            sin = sin_ref[0, rows, :]
            v1, v2 = v[:, :half], v[:, half:]
            return jnp.concatenate([v1 * cos - v2 * sin, v2 * cos + v1 * sin], axis=1)

        for c in range(nchunk):
            @piece(0.1)
            def _(c=c):
                rows = slice(HALO + c * CHUNK, HALO + (c + 1) * CHUNK)
                pbuf_ref[rows] = z(9, c) * z(10, c)
                if c == 0:
                    pbuf_ref[0:HALO] = zr_ref[0:HALO, _cols(9)] * zr_ref[0:HALO, _cols(10)]
                if c == nchunk - 1:
                    tail = slice(HALO + n, HALO + n + HALO)
                    pbuf_ref[tail] = zr_ref[tail, _cols(9)] * zr_ref[tail, _cols(10)]

        for c in range(nchunk - 1, -1, -1):
            rows = slice(c * CHUNK, (c + 1) * CHUNK)

            @piece(0.9)
            def _(rows=rows, c=c):
                xf = cb_ref[...]
                for j in range(4):
                    xf = xf + cw_ref[j:j + 1, :] * z(6, c, j - 2)
                st["xf"] = xf
                st["xf16"] = xf.astype(BF16)
                zp_ref[0, rows, _cols(P_XF)] = st["xf16"]
                zp_ref[0, rows, _cols(P_GATE)] = z(7, c).astype(BF16)

            @piece(0.8)
            def _():
                st["ab"] = _lru_coeffs(st["xf"], st["xf16"], wg_ref, bg_ref, lam_ref)

            @piece(0.9)
            def _(rows=rows, c=c):
                carry = hc_ref[...] if st["carry"] is None else st["carry"]
                hb, st["carry"] = _linear_scan(*st["ab"], carry, reverse=True)
                if c == 0:
                    hc_ref[...] = st["carry"]
                zp_ref[0, rows, _cols(P_HB)] = hb.astype(BF16)

            @piece(0.8)
            def _(rows=rows, c=c):
                q = rotary(z(2, c), rows)
                k = rotary(z(3, c), rows) * HEAD_DIM ** -0.5
                vb = z(4, c).astype(BF16)
                zp_ref[0, rows, _cols(P_Q)] = q.astype(BF16)
                zp_ref[0, rows, _cols(P_K)] = k.astype(BF16)
                zp_ref[0, rows, _cols(P_V)] = vb
                zp_ref[0, rows, _cols(P_G)] = z(5, c).astype(BF16)
                state = l_ref[...]
                ob = _dot((q * qdec_ref[...]).astype(BF16), state.astype(BF16))
                zp_ref[0, rows, _cols(P_OB)] = ob.astype(BF16)
                upd = _dot_tn((k * kdec_ref[...]).astype(BF16), vb)
                l_ref[...] = dm_ref[...] * state + bd_ref[...] * upd

            @piece(1.2)
            def _(rows=rows, c=c):
                u = jax.nn.gelu(z(0, c))
                vc = _layer_norm(jax.nn.gelu(z(1, c)), lng_ref[...], lnb_ref[...])
                vstack = jnp.concatenate([jnp.where(m, vc, 0.0) for m in nat], axis=0).astype(BF16)
                mixed = _dot(ws_ref[...], vstack) + bs_ref[...]
                zp_ref[0, rows, _cols(P_YA)] = (u * mixed).astype(BF16)

            @piece(0.35)
            def _(rows=rows, c=c):
                conv = scw_ref[0:1, :] * pbuf_ref[pl.ds(HALO + c * CHUNK - 1, CHUNK), :]
                for j in range(1, 3):
                    conv = conv + scw_ref[j:j + 1, :] * pbuf_ref[pl.ds(HALO + c * CHUNK + j - 1, CHUNK), :]
                zp_ref[0, rows, _cols(P_YD)] = (z(8, c) * conv).astype(BF16)

        for run in _merge_by_weight(producers, consumers):
            run()

    parity = lax.rem(s, 2)
    pl.when(parity == 0)(functools.partial(step, za_ref, zb_ref))
    pl.when(parity == 1)(functools.partial(step, zb_ref, za_ref))


def _mix_bwd(x, cos, sin, w, lng, lnb, ws, bs, cw, cb, wg, bg, lam, scw, kdec, qdec, dm, bd):
    bsz, s, d = x.shape
    nt = s // TILE
    n_tiles = bsz * nt
    hb = TILE // HALO
    half = GROUP_WIDTH // 2

    def full(a):
        return pl.BlockSpec(a.shape, lambda i: (0,) * a.ndim)

    def seq_tile(j):
        return nt - 1 - j % nt

    def projected(i):
        j = jnp.minimum(i, n_tiles - 1)
        return (j // nt, seq_tile(j), 0)

    def before(i):
        j = jnp.minimum(i, n_tiles - 1)
        return (j // nt, jnp.maximum(seq_tile(j) * hb - 1, 0), 0)

    def after(i):
        j = jnp.minimum(i, n_tiles - 1)
        return (j // nt, jnp.minimum((seq_tile(j) + 1) * hb, nt * hb - 1), 0)

    def mixed(i):
        j = jnp.maximum(i - 1, 0)
        return (j // nt, seq_tile(j), 0)

    in_specs = [
        pl.BlockSpec((1, TILE, d), projected),
        pl.BlockSpec((1, HALO, d), before),
        pl.BlockSpec((1, HALO, d), after),
        pl.BlockSpec((1, TILE, half), mixed),
        pl.BlockSpec((1, TILE, half), mixed),
    ] + [full(a) for a in (w, lng, lnb, ws, bs, cw, cb, wg, bg, lam, scw, kdec, qdec, dm, bd)]
    return pl.pallas_call(
        functools.partial(_mix_bwd_kernel, nt),
        grid=(n_tiles + 1,),
        in_specs=in_specs,
        out_specs=pl.BlockSpec((1, TILE, PACK_COLS), mixed),
        out_shape=jax.ShapeDtypeStruct((bsz, s, PACK_COLS), BF16),
        scratch_shapes=[
            pltpu.VMEM((GROUP_WIDTH, GROUP_WIDTH), F32),
            pltpu.VMEM((1, GROUP_WIDTH), F32),
            pltpu.VMEM((TILE + 2 * HALO, GROUP_WIDTH), F32),
            pltpu.VMEM((TILE + 2 * HALO, N_SLICES * GROUP_WIDTH), F32),
            pltpu.VMEM((TILE + 2 * HALO, N_SLICES * GROUP_WIDTH), F32),
        ],
        compiler_params=pltpu.CompilerParams(
            dimension_semantics=("arbitrary",),
            vmem_limit_bytes=VMEM_LIMIT_BYTES),
        name="mix_bwd",
    )(x, x, x, cos, sin, w, lng, lnb, ws, bs, cw, cb, wg, bg, lam, scw, kdec, qdec, dm, bd)


def _mix_fwd_kernel(alpha, ff_width, nt,
                    x_ref, zp_ref, intra_ref, kdec_ref, qdec_ref, dm_ref, bd_ref, avg_ref,
                    gng_ref, gnb_ref, wg_ref, bg_ref, lam_ref,
                    wo_ref, l1g_ref, l1b_ref, fg_ref, fu_ref, fd_ref, l2g_ref, l2b_ref,
                    o_ref, r_ref, hc_ref, x1_ref):
    s = pl.program_id(0)
    n_tiles = pl.num_programs(0) - 1
    n = x_ref.shape[1]
    nchunk = n // CHUNK
    t_in = lax.rem(jnp.minimum(s, n_tiles - 1), nt)
    slot = lax.rem(s, 2)

    @pl.when(s == 0)
    def _():
        x1_ref[...] = jnp.zeros_like(x1_ref)

    @pl.when(t_in == 0)
    def _():
        r_ref[...] = jnp.zeros_like(r_ref)
        hc_ref[...] = jnp.zeros_like(hc_ref)

    ffn = []
    st = {"acc": None}
    x1b = x1_ref[1 - slot].astype(BF16)
    d_ff = fg_ref.shape[1]

    def ffn_up(cs):
        def run():
            st["g"] = _dot(x1b, fg_ref[:, cs])
            st["u"] = _dot(x1b, fu_ref[:, cs])
        return run

    def ffn_down(cs):
        def run():
            hid = (jax.nn.silu(st["g"]) * st["u"]).astype(BF16)
            part = _dot(hid, fd_ref[cs, :])
            st["acc"] = part if st["acc"] is None else st["acc"] + part
        return run

    for j in range(0, d_ff, ff_width):
        cs = slice(j, min(j + ff_width, d_ff))
        ffn += [ffn_up(cs), ffn_down(cs)]

    def ffn_norm():
        o_ref[0] = _layer_norm(alpha * x1_ref[1 - slot] + st["acc"], l2g_ref[...], l2b_ref[...])

    nat = _lane_head_masks(permuted=False)
    per = _lane_head_masks(permuted=True)
    mix = []
    ms = {"carry": None, "yb": [None] * nchunk, "yc": [None] * nchunk}

    def ret_scores(c):
        def run():
            rows = slice(c * CHUNK, (c + 1) * CHUNK)
            qc = zp_ref[0, rows, _cols(P_Q)].astype(F32)
            k16 = zp_ref[0, rows, _cols(P_K)]
            scores = []
            for h in range(HEADS):
                sh = _dot_nt(jnp.where(per[h], qc, 0.0).astype(BF16), k16)
                scores.append(sh * intra_ref[:, h * CHUNK:(h + 1) * CHUNK])
            ms["scores"] = jnp.concatenate(scores, axis=1).astype(BF16)
            ms["qc"] = qc
        return run

    def ret_out(c):
        def run():
            rows = slice(c * CHUNK, (c + 1) * CHUNK)
            k16 = zp_ref[0, rows, _cols(P_K)]
            v16 = zp_ref[0, rows, _cols(P_V)]
            vc = v16.astype(F32)
            vstack = jnp.concatenate([jnp.where(m, vc, 0.0) for m in nat], axis=0).astype(BF16)
            state = r_ref[...]
            ms["out"] = (_dot(ms["scores"], vstack)
                         + _dot((ms["qc"] * qdec_ref[...]).astype(BF16), state.astype(BF16))
                         + zp_ref[0, rows, _cols(P_OB)].astype(F32))
            upd = _dot_tn((k16.astype(F32) * kdec_ref[...]).astype(BF16), v16)
            r_ref[...] = dm_ref[...] * state + bd_ref[...] * upd
        return run

    def ret_norm(c):
        def run():
            rows = slice(c * CHUNK, (c + 1) * CHUNK)
            out = ms["out"]
            mu = _dot(out.astype(BF16), avg_ref[...])
            dlt = out - mu
            var = _dot((dlt * dlt).astype(BF16), avg_ref[...])
            normed = dlt * lax.rsqrt(var + LN_EPS) * gng_ref[...] + gnb_ref[...]
            g = zp_ref[0, rows, _cols(P_G)].astype(F32)
            ms["yb"][c] = (jax.nn.silu(g) * normed).astype(BF16)
        return run

    def lru_coeffs(c):
        def run():
            rows = slice(c * CHUNK, (c + 1) * CHUNK)
            xf16 = zp_ref[0, rows, _cols(P_XF)]
            ms["ab"] = _lru_coeffs(xf16.astype(F32), xf16, wg_ref, bg_ref, lam_ref)
        return run

    def lru_scan(c):
        def run():
            rows = slice(c * CHUNK, (c + 1) * CHUNK)
            carry = hc_ref[...] if ms["carry"] is None else ms["carry"]
            hf, ms["carry"] = _linear_scan(*ms["ab"], carry, reverse=False)
            if c == nchunk - 1:
                hc_ref[...] = ms["carry"]
            gate = zp_ref[0, rows, _cols(P_GATE)].astype(F32)
            hb = zp_ref[0, rows, _cols(P_HB)].astype(F32)
            ms["yc"][c] = ((hf + hb) * jax.nn.gelu(gate)).astype(BF16)
        return run

    def out_proj(c0, c1):
        def run():
            rows = slice(c0 * CHUNK, c1 * CHUNK)
            y = jnp.concatenate([zp_ref[0, rows, _cols(P_YA)],
                                 jnp.concatenate(ms["yb"][c0:c1], axis=0),
                                 jnp.concatenate(ms["yc"][c0:c1], axis=0),
                                 zp_ref[0, rows, _cols(P_YD)]], axis=1)
            x1_ref[slot, rows, :] = _layer_norm(alpha * x_ref[0, rows, :] + _dot(y, wo_ref[...]),
                                                l1g_ref[...], l1b_ref[...])
        return run

    per_proj = 2 if nchunk % 2 == 0 else 1
    for c in range(nchunk):
        mix += [ret_scores(c), ret_out(c), ret_norm(c), lru_coeffs(c), lru_scan(c)]
        if (c + 1) % per_proj == 0:
            mix.append(out_proj(c + 1 - per_proj, c + 1))

    split = len(ffn) * 3 // 4
    for piece in _merge_evenly(ffn[:split], mix) + ffn[split:] + [ffn_norm]:
        piece()


def _mix_fwd(alpha, x, zp, intra, kdec, qdec, dm, bd, avg, gng, gnb, wg, bg, lam,
             wo, l1g, l1b, fg, fu, fd, l2g, l2b):
    bsz, s, d = x.shape
    nt = s // TILE
    n_tiles = bsz * nt

    def full(a):
        return pl.BlockSpec(a.shape, lambda i: (0,) * a.ndim, pipeline_mode=pl.Buffered(1))

    def tile_in(i):
        j = jnp.minimum(i, n_tiles - 1)
        return (j // nt, j % nt, 0)

    def tile_out(i):
        j = jnp.maximum(i - 1, 0)
        return (j // nt, j % nt, 0)

    consts = (intra, kdec, qdec, dm, bd, avg, gng, gnb, wg, bg, lam, wo, l1g, l1b, fg, fu, fd, l2g, l2b)
    in_specs = [
        pl.BlockSpec((1, TILE, d), tile_in),
        pl.BlockSpec((1, TILE, PACK_COLS), tile_in),
    ] + [full(a) for a in consts]
    return pl.pallas_call(
        functools.partial(_mix_fwd_kernel, alpha, MXU_WIDTH, nt),
        grid=(n_tiles + 1,),
        in_specs=in_specs,
        out_specs=pl.BlockSpec((1, TILE, d), tile_out),
        out_shape=jax.ShapeDtypeStruct((bsz, s, d), F32),
        scratch_shapes=[
            pltpu.VMEM((GROUP_WIDTH, GROUP_WIDTH), F32),
            pltpu.VMEM((1, GROUP_WIDTH), F32),
            pltpu.VMEM((2, TILE, d), F32),
        ],
        compiler_params=pltpu.CompilerParams(
            dimension_semantics=("arbitrary",),
            vmem_limit_bytes=VMEM_LIMIT_BYTES),
        name="mix_fwd",
    )(x, zp, *consts)


def _permuted_head_of_lane():
    lane = np.arange(GROUP_WIDTH)
    return (lane % (GROUP_WIDTH // 2)) // HALF_DIM


def _retention_tables():
    log_gamma = np.log1p(-np.exp2(-5.0 - np.arange(HEADS, dtype=np.float64)))
    idx = np.arange(CHUNK, dtype=np.float64)
    intra = np.exp(log_gamma[:, None, None] * np.abs(idx[:, None] - idx[None, :]))
    intra = np.concatenate(list(intra), axis=1)
    lg_lane = log_gamma[_permuted_head_of_lane()]
    kdec_f = np.exp(lg_lane[None, :] * (CHUNK - 1 - idx)[:, None])
    kdec_b = np.exp(lg_lane[None, :] * idx[:, None])
    qdec_f = np.exp(lg_lane[None, :] * (idx + 1.0)[:, None])
    qdec_b = np.exp(lg_lane[None, :] * (CHUNK - idx)[:, None])
    same = _permuted_head_of_lane()[:, None] == (np.arange(GROUP_WIDTH) // HEAD_DIM)[None, :]
    dm = np.where(same, np.exp(lg_lane * CHUNK)[:, None], 0.0)
    bd = same.astype(np.float64)
    nat_head = np.arange(GROUP_WIDTH) // HEAD_DIM
    avg = (nat_head[:, None] == nat_head[None, :]) / HEAD_DIM
    f = lambda a: jnp.asarray(a, F32)
    return dict(intra=f(intra), kdec_f=f(kdec_f), kdec_b=f(kdec_b), qdec_f=f(qdec_f), qdec_b=f(qdec_b),
                dm=f(dm), bd=f(bd), avg=jnp.asarray(avg, BF16))


def _block_diag(w):
    eye = jnp.eye(HEADS, dtype=w.dtype)
    return jnp.einsum('hij,hg->higj', w, eye).reshape(GROUP_WIDTH, GROUP_WIDTH)


def kernel(x, positions, w_in, gmlp_ln_g, gmlp_ln_b, gmlp_ws, gmlp_bs, ret_gn_g, ret_gn_b, lru_conv_w, lru_conv_b, lru_wa, lru_ba, lru_wx, lru_bx, lru_lambda, sc_conv_w, w_out, ln1_g, ln1_b, ffn_wg, ffn_wu, ffn_wd, ln2_g, ln2_b):
    depth = w_in.shape[0]
    assert x.shape[1] % TILE == 0 and x.shape[2] % GROUP_WIDTH == 0
    assert w_in.shape[2] == N_SLICES * GROUP_WIDTH
    alpha = (2 * depth) ** 0.25
    tb = _retention_tables()

    inv_freq = ROPE_BASE ** (-np.arange(0, HEAD_DIM, 2, dtype=np.float64) / HEAD_DIM)
    inv_row = jnp.asarray(np.tile(inv_freq, HEADS)[None, :], F32)
    cos, sin = _rope_tables(positions, inv_row)

    def halves_first(w):
        lead = w.shape[:-1]
        w = w.reshape(*lead, HEADS, 2, HALF_DIM)
        return jnp.swapaxes(w, -3, -2).reshape(*lead, GROUP_WIDTH)

    w_in_bf16 = jnp.concatenate(
        [w_in[..., _cols(0, 2)], halves_first(w_in[..., _cols(2)]), halves_first(w_in[..., _cols(3)]),
         w_in[..., _cols(4, N_SLICES - 4)]], axis=-1).astype(BF16)
    w_out_bf16 = w_out.astype(BF16)
    ffn_wg_bf16, ffn_wu_bf16, ffn_wd_bf16 = (a.astype(BF16) for a in (ffn_wg, ffn_wu, ffn_wd))
    row = lambda a: a.reshape(1, -1)

    for l in range(depth):
        w = w_in_bf16[l]
        ws = jnp.transpose(gmlp_ws[l], (1, 0, 2)).reshape(CHUNK, HEADS * CHUNK).astype(BF16)
        bs = jnp.repeat(gmlp_bs[l].T, HEAD_DIM, axis=1)
        gate_w = [jnp.concatenate([_block_diag(lru_wa[l, z]), _block_diag(lru_wx[l, z])], axis=1).astype(BF16)
                  for z in range(2)]
        gate_b = [jnp.concatenate([lru_ba[l, z], lru_bx[l, z]])[None, :] for z in range(2)]
        zp = _mix_bwd(x, cos, sin, w, row(gmlp_ln_g[l]), row(gmlp_ln_b[l]), ws, bs,
                      lru_conv_w[l], row(lru_conv_b[l]), gate_w[1], gate_b[1], row(lru_lambda[l, 1]),
                      sc_conv_w[l], tb['kdec_b'], tb['qdec_b'], tb['dm'], tb['bd'])
        x = _mix_fwd(alpha, x, zp, tb['intra'], tb['kdec_f'], tb['qdec_f'], tb['dm'], tb['bd'], tb['avg'],
                     row(ret_gn_g[l]), row(ret_gn_b[l]), gate_w[0], gate_b[0], row(lru_lambda[l, 0]),
                     w_out_bf16[l], row(ln1_g[l]), row(ln1_b[l]),
                     ffn_wg_bf16[l], ffn_wu_bf16[l], ffn_wd_bf16[l],
                     row(ln2_g[l]), row(ln2_b[l]))
    return x
```
